```python
import jax, jax.numpy as jnp
from jax import lax
import numpy as np

D_MODEL = 1024
BATCH = 8
SEQ = 4096
DEPTH = 1

D_MIX = D_MODEL
C_CONV = D_MIX // 2
C_POOL = D_MIX - C_CONV
CONV_HEADS = 8
CONV_WIDTH = 31
POOL_WINDOWS = (2, 4, 8, 16)
N_POOL_GROUPS = len(POOL_WINDOWS)
POOL_GROUP = C_POOL // N_POOL_GROUPS
D_IN = 2 * C_CONV + C_POOL
D_FF = ((8 * D_MODEL // 3 + 255) // 256) * 256
RMS_EPS = 1e-6
LN_EPS = 1e-5

kernel_name = "hybrid_conformer_conv_multiscale_pool_block"


def rmsnorm(x, g):
    xf = x.astype(jnp.float32)
    y = xf * lax.rsqrt(jnp.mean(xf * xf, axis=-1, keepdims=True) + RMS_EPS)
    return (y * g.astype(jnp.float32)).astype(x.dtype)


def layernorm(x, g, b):
    xf = x.astype(jnp.float32)
    mu = jnp.mean(xf, axis=-1, keepdims=True)
    var = jnp.mean(jnp.square(xf - mu), axis=-1, keepdims=True)
    y = (xf - mu) * lax.rsqrt(var + LN_EPS)
    return (y * g.astype(jnp.float32) + b.astype(jnp.float32)).astype(x.dtype)


def conformer_conv_group(a, gate, w_dw, b_dw, ln_g, ln_b):
    u = a * jax.nn.sigmoid(gate)
    k = w_dw.astype(u.dtype)[:, None, :]
    v = lax.conv_general_dilated(
        u, k, window_strides=(1,), padding=[(CONV_WIDTH - 1, 0)],
        dimension_numbers=("NWC", "WIO", "NWC"),
        feature_group_count=C_CONV) + b_dw.astype(u.dtype)
    v = layernorm(v, ln_g, ln_b)
    return jax.nn.silu(v)


def multiscale_pool_group(p, w_pool, s_pool):
    seq = p.shape[1]
    pos = jnp.arange(seq)
    outs = []
    for i, w in enumerate(POOL_WINDOWS):
        seg = p[..., i * POOL_GROUP:(i + 1) * POOL_GROUP].astype(jnp.float32)
        cs = jnp.cumsum(seg, axis=1)
        lag = jnp.pad(cs, ((0, 0), (w, 0), (0, 0)))[:, :seq]
        cnt = jnp.minimum(pos + 1, w).astype(jnp.float32)[None, :, None]
        y = ((cs - lag) / cnt - seg).astype(p.dtype)
        outs.append(jnp.einsum("bsg,gh->bsh", y, w_pool[i]))
    return jnp.concatenate(outs, axis=-1) * s_pool


def swiglu(x, w_gate, w_up, w_down):
    return (jax.nn.silu(x @ w_gate) * (x @ w_up)) @ w_down


def setup_inputs(seed: int = 0) -> dict:
    key = jax.random.key(seed)
    ks = jax.random.split(key, 20)
    f = jnp.float32
    n = lambda k, s, sc: jax.random.normal(k, s, f) * sc
    return {
        "x": jax.random.normal(ks[0], (BATCH, SEQ, D_MODEL), f),
        "g_mix": 1.0 + n(ks[1], (DEPTH, D_MODEL), 0.05),
        "w_in": n(ks[2], (DEPTH, D_MODEL, D_IN), D_MODEL ** -0.5),
        "b_in": n(ks[3], (DEPTH, D_IN), 0.02),
        "w_dw": n(ks[4], (DEPTH, CONV_WIDTH, C_CONV), CONV_WIDTH ** -0.5),
        "b_dw": n(ks[5], (DEPTH, C_CONV), 0.02),
        "ln_g": 1.0 + n(ks[6], (DEPTH, C_CONV), 0.05),
        "ln_b": n(ks[7], (DEPTH, C_CONV), 0.02),
        "w_pool": n(ks[8], (DEPTH, N_POOL_GROUPS, POOL_GROUP, POOL_GROUP), POOL_GROUP ** -0.5),
        "s_pool": 1.0 + n(ks[9], (DEPTH, C_POOL), 0.1),
        "w_out": n(ks[10], (DEPTH, D_MIX, D_MODEL), D_MIX ** -0.5),
        "g_ffn": 1.0 + n(ks[11], (DEPTH, D_MODEL), 0.05),
        "w_gate": n(ks[12], (DEPTH, D_MODEL, D_FF), D_MODEL ** -0.5),
        "w_up": n(ks[13], (DEPTH, D_MODEL, D_FF), D_MODEL ** -0.5),
        "w_down": n(ks[14], (DEPTH, D_FF, D_MODEL), D_FF ** -0.5),
        "g_final": 1.0 + n(ks[15], (D_MODEL,), 0.05),
    }


def reference(x, g_mix, w_in, b_in, w_dw, b_dw, ln_g, ln_b, w_pool, s_pool,
              w_out, g_ffn, w_gate, w_up, w_down, g_final):
    h = x
    for l in range(DEPTH):
        xn = rmsnorm(h, g_mix[l])
        z = xn @ w_in[l] + b_in[l]
        a = z[..., :C_CONV]
        gate = z[..., C_CONV:2 * C_CONV]
        p = z[..., 2 * C_CONV:]
        y_conv = conformer_conv_group(a, gate, w_dw[l], b_dw[l], ln_g[l], ln_b[l])
        y_pool = multiscale_pool_group(p, w_pool[l], s_pool[l])
        y = jnp.concatenate([y_conv, y_pool], axis=-1)
        h = h + y @ w_out[l]
        h = h + swiglu(rmsnorm(h, g_ffn[l]), w_gate[l], w_up[l], w_down[l])
    return rmsnorm(h, g_final)
```

```python
import functools

import jax
import jax.numpy as jnp
from jax import lax
from jax.experimental import pallas as pl
from jax.experimental.pallas import tpu as pltpu

RMS_EPS = 1e-6
LN_EPS = 1e-5
POOL_WINDOWS = (2, 4, 8, 16)

LANES = 128
SUBLANES = 8
MXU_N = 256
VMEM_LIMIT_BYTES = 56 * 1024 * 1024

SEQ_TILE = 512
ROW_CHUNK = 32

F32 = jnp.float32
BF16 = jnp.bfloat16


def _rmsnorm_rows(v, g):
    ms = jnp.mean(v * v, axis=-1, keepdims=True)
    return v * lax.rsqrt(ms + RMS_EPS) * g


def _layer_kernel(x_ref, gmix_ref, win_ref, bin_ref, wdw_ref, bdw_ref, lng_ref, lnb_ref,
                  wpool_ref, spool_ref, wout_ref, gffn_ref, wg_ref, wu_ref, wd_ref, gfin_ref,
                  o_ref,
                  xn_buf, ubuf, pbuf, ypin, ycat, h_buf, hid_buf,
                  *, apply_final, conv_width, halo_u, halo_p):
    ts = x_ref.shape[1]
    d_model = x_ref.shape[2]
    c_conv = ubuf.shape[1]
    c_pool = pbuf.shape[1]
    d_ff = hid_buf.shape[1]
    pool_group = c_pool // len(POOL_WINDOWS)
    s_idx = pl.program_id(1)

    @pl.when(s_idx == 0)
    def _():
        ubuf[0:halo_u, :] = jnp.zeros((halo_u, c_conv), F32)
        pbuf[0:halo_p, :] = jnp.zeros((halo_p, c_pool), F32)

    for r0 in range(0, ts, ROW_CHUNK):
        rows = slice(r0, r0 + ROW_CHUNK)
        xn_buf[rows, :] = _rmsnorm_rows(x_ref[0, rows, :], gmix_ref[...]).astype(BF16)

    def z_cols(c0):
        cols = slice(c0, c0 + MXU_N)
        return jnp.dot(xn_buf[...], win_ref[:, cols], preferred_element_type=F32) + bin_ref[:, cols]

    for c0 in range(0, c_conv, MXU_N):
        a = z_cols(c0)
        gate = z_cols(c_conv + c0)
        ubuf[halo_u:halo_u + ts, c0:c0 + MXU_N] = a * jax.nn.sigmoid(gate)
    for c0 in range(0, c_pool, MXU_N):
        pbuf[halo_p:halo_p + ts, c0:c0 + MXU_N] = z_cols(2 * c_conv + c0)

    base = halo_u - (conv_width - 1)
    for r0 in range(0, ts, ROW_CHUNK):
        v = None
        for phase in range(SUBLANES):
            acc = None
            for k in range(conv_width):
                if (base + k) % SUBLANES != phase:
                    continue
                term = wdw_ref[k:k + 1, :] * ubuf[pl.ds(r0 + base + k, ROW_CHUNK), :]
                acc = term if acc is None else acc + term
            if acc is not None:
                v = acc if v is None else v + acc
        v = v + bdw_ref[...]
        mu = jnp.mean(v, axis=-1, keepdims=True)
        dv = v - mu
        var = jnp.mean(dv * dv, axis=-1, keepdims=True)
        yn = dv * lax.rsqrt(var + LN_EPS) * lng_ref[...] + lnb_ref[...]
        ycat[r0:r0 + ROW_CHUNK, 0:c_conv] = (yn * jax.nn.sigmoid(yn)).astype(BF16)

    for r0 in range(0, ts, ROW_CHUNK):
        pos = s_idx * ts + r0 + lax.broadcasted_iota(jnp.int32, (ROW_CHUNK, pool_group), 0)
        for gi, w in enumerate(POOL_WINDOWS):
            cols = slice(gi * pool_group, (gi + 1) * pool_group)
            cur = pbuf[pl.ds(halo_p + r0, ROW_CHUNK), cols]
            ssum = cur
            for j in range(1, w):
                ssum = ssum + pbuf[pl.ds(halo_p + r0 - j, ROW_CHUNK), cols]
            cnt = jnp.minimum(pos + 1, w).astype(F32)
            ypin[r0:r0 + ROW_CHUNK, cols] = (ssum / cnt - cur).astype(BF16)
    for gi in range(len(POOL_WINDOWS)):
        cols = slice(gi * pool_group, (gi + 1) * pool_group)
        yo = jnp.dot(ypin[:, cols], wpool_ref[gi], preferred_element_type=F32) * spool_ref[:, cols]
        ycat[:, c_conv + gi * pool_group:c_conv + (gi + 1) * pool_group] = yo.astype(BF16)

    ubuf[0:halo_u, :] = ubuf[ts:ts + halo_u, :]
    pbuf[0:halo_p, :] = pbuf[ts:ts + halo_p, :]

    for c0 in range(0, d_model, MXU_N):
        cols = slice(c0, c0 + MXU_N)
        h_buf[:, cols] = x_ref[0, :, cols] + jnp.dot(ycat[...], wout_ref[:, cols],
                                                     preferred_element_type=F32)

    for r0 in range(0, ts, ROW_CHUNK):
        rows = slice(r0, r0 + ROW_CHUNK)
        xn_buf[rows, :] = _rmsnorm_rows(h_buf[rows, :], gffn_ref[...]).astype(BF16)

    for c0 in range(0, d_ff, MXU_N):
        cols = slice(c0, c0 + MXU_N)
        g = jnp.dot(xn_buf[...], wg_ref[:, cols], preferred_element_type=F32)
        u = jnp.dot(xn_buf[...], wu_ref[:, cols], preferred_element_type=F32)
        hid_buf[:, cols] = (g * jax.nn.sigmoid(g) * u).astype(BF16)

    for c0 in range(0, d_model, MXU_N):
        cols = slice(c0, c0 + MXU_N)
        h_buf[:, cols] = h_buf[:, cols] + jnp.dot(hid_buf[...], wd_ref[:, cols],
                                                  preferred_element_type=F32)

    for r0 in range(0, ts, ROW_CHUNK):
        rows = slice(r0, r0 + ROW_CHUNK)
        hr = h_buf[rows, :]
        o_ref[0, rows, :] = _rmsnorm_rows(hr, gfin_ref[...]) if apply_final else hr


def _resident(shape):
    zeros = (0,) * len(shape)
    return pl.BlockSpec(shape, lambda b, s: zeros, pipeline_mode=pl.Buffered(1))


def _layer(h, g_mix, w_in, b_in, w_dw, b_dw, ln_g, ln_b, w_pool, s_pool, w_out, g_ffn,
           w_gate, w_up, w_down, g_final, *, apply_final):
    batch, seq, d_model = h.shape
    conv_width, c_conv = w_dw.shape
    c_pool = s_pool.shape[0]
    d_ff = w_gate.shape[1]
    ts = SEQ_TILE
    halo_u = -(-(conv_width - 1) // ROW_CHUNK) * ROW_CHUNK
    halo_p = -(-(max(POOL_WINDOWS) - 1) // 16) * 16
    assert seq % ts == 0 and ts % ROW_CHUNK == 0 and ts >= halo_u
    assert c_conv % MXU_N == 0 and c_pool % MXU_N == 0 and d_model % MXU_N == 0 and d_ff % MXU_N == 0
    assert c_pool == len(POOL_WINDOWS) * LANES

    row = lambda v: v.reshape(1, -1).astype(F32)
    operands = (
        h,
        row(g_mix), w_in.astype(BF16), row(b_in),
        w_dw.astype(F32), row(b_dw), row(ln_g), row(ln_b),
        w_pool.astype(BF16), row(s_pool),
        w_out.astype(BF16), row(g_ffn),
        w_gate.astype(BF16), w_up.astype(BF16), w_down.astype(BF16),
        row(g_final),
    )
    tile = pl.BlockSpec((1, ts, d_model), lambda b, s: (b, s, 0))
    in_specs = [tile] + [_resident(op.shape) for op in operands[1:]]

    body = functools.partial(_layer_kernel, apply_final=apply_final, conv_width=conv_width,
                             halo_u=halo_u, halo_p=halo_p)
    return pl.pallas_call(
        body,
        grid=(batch, seq // ts),
        in_specs=in_specs,
        out_specs=tile,
        out_shape=jax.ShapeDtypeStruct(h.shape, h.dtype),
        scratch_shapes=[
            pltpu.VMEM((ts, d_model), BF16),
            pltpu.VMEM((halo_u + ts, c_conv), F32),
            pltpu.VMEM((halo_p + ts, c_pool), F32),
            pltpu.VMEM((ts, c_pool), BF16),
            pltpu.VMEM((ts, c_conv + c_pool), BF16),
            pltpu.VMEM((ts, d_model), F32),
            pltpu.VMEM((ts, d_ff), BF16),
        ],
        compiler_params=pltpu.CompilerParams(
            dimension_semantics=("arbitrary", "arbitrary"),
            vmem_limit_bytes=VMEM_LIMIT_BYTES,
        ),
        name="hybrid_block_layer",
    )(*operands)


def kernel(x, g_mix, w_in, b_in, w_dw, b_dw, ln_g, ln_b, w_pool, s_pool, w_out, g_ffn,
           w_gate, w_up, w_down, g_final):
    depth = w_in.shape[0]
    assert depth >= 1
    h = x
    for l in range(depth):
        h = _layer(h, g_mix[l], w_in[l], b_in[l], w_dw[l], b_dw[l], ln_g[l], ln_b[l],
                   w_pool[l], s_pool[l], w_out[l], g_ffn[l], w_gate[l], w_up[l], w_down[l],
                   g_final, apply_final=(l == depth - 1))
    return h
```

```python
import functools

import jax
import jax.numpy as jnp
from jax import lax
from jax.experimental import pallas as pl
from jax.experimental.pallas import tpu as pltpu

RMS_EPS = 1e-6
LN_EPS = 1e-5
POOL_WINDOWS = (2, 4, 8, 16)

LANES = 128
SUBLANES = 8
MXU_N = 256
VMEM_LIMIT_BYTES = 56 * 1024 * 1024

SEQ_TILE = 512
ROW_CHUNK = 32
CONV_ROWS = 64

F32 = jnp.float32
BF16 = jnp.bfloat16


def _rmsnorm_rows(v, g):
    ms = jnp.mean(v * v, axis=-1, keepdims=True)
    return v * lax.rsqrt(ms + RMS_EPS) * g


def _layer_kernel(x_ref, gmix_ref, win_ref, bin_ref, wdw_ref, bdw_ref, lng_ref, lnb_ref,
                  wpool_ref, spool_ref, wout_ref, gffn_ref, wg_ref, wu_ref, wd_ref, gfin_ref,
                  o_ref,
                  xn_buf, ubuf, pbuf, ypin, ycat, h_buf, hid_buf,
                  *, apply_final, conv_width, halo_u, halo_p):
    ts = x_ref.shape[1]
    d_model = x_ref.shape[2]
    n_uslab, n_pslab = ubuf.shape[0], pbuf.shape[0]
    c_conv, c_pool = n_uslab * LANES, n_pslab * LANES
    d_ff = hid_buf.shape[1]
    s_idx = pl.program_id(1)

    @pl.when(s_idx == 0)
    def _():
        ubuf[:, 0:halo_u, :] = jnp.zeros((n_uslab, halo_u, LANES), F32)
        pbuf[:, 0:halo_p, :] = jnp.zeros((n_pslab, halo_p, LANES), F32)

    for r0 in range(0, ts, ROW_CHUNK):
        rows = slice(r0, r0 + ROW_CHUNK)
        xn_buf[rows, :] = _rmsnorm_rows(x_ref[0, rows, :], gmix_ref[...]).astype(BF16)

    def z_cols(c0):
        cols = slice(c0, c0 + MXU_N)
        return jnp.dot(xn_buf[...], win_ref[:, cols], preferred_element_type=F32) + bin_ref[:, cols]

    def store_slabs(buf, halo, c0, val):
        for l0 in range(0, MXU_N, LANES):
            buf[(c0 + l0) // LANES, halo:halo + ts, :] = val[:, l0:l0 + LANES]

    for c0 in range(0, c_conv, MXU_N):
        a = z_cols(c0)
        gate = z_cols(c_conv + c0)
        store_slabs(ubuf, halo_u, c0, a * jax.nn.sigmoid(gate))
    for c0 in range(0, c_pool, MXU_N):
        store_slabs(pbuf, halo_p, c0, z_cols(2 * c_conv + c0))

    base = halo_u - (conv_width - 1)
    for r0 in range(0, ts, CONV_ROWS):
        slabs = []
        for sl in range(n_uslab):
            lanes = slice(sl * LANES, (sl + 1) * LANES)
            acc = None
            for k in range(conv_width):
                term = wdw_ref[k:k + 1, lanes] * ubuf[sl, pl.ds(r0 + base + k, CONV_ROWS), :]
                acc = term if acc is None else acc + term
            slabs.append(acc)
        v = jnp.concatenate(slabs, axis=1) + bdw_ref[...]
        mu = jnp.mean(v, axis=-1, keepdims=True)
        dv = v - mu
        var = jnp.mean(dv * dv, axis=-1, keepdims=True)
        yn = dv * lax.rsqrt(var + LN_EPS) * lng_ref[...] + lnb_ref[...]
        ycat[r0:r0 + CONV_ROWS, 0:c_conv] = (yn * jax.nn.sigmoid(yn)).astype(BF16)

    for r0 in range(0, ts, ROW_CHUNK):
        for gi, w in enumerate(POOL_WINDOWS):
            cur = pbuf[gi, pl.ds(halo_p + r0, ROW_CHUNK), :]
            ssum = cur
            for j in range(1, w):
                ssum = ssum + pbuf[gi, pl.ds(halo_p + r0 - j, ROW_CHUNK), :]
            if r0 == 0:
                pos = s_idx * ts + lax.broadcasted_iota(jnp.int32, (ROW_CHUNK, LANES), 0)
                mean = ssum / jnp.minimum(pos + 1, w).astype(F32)
            elif w & (w - 1) == 0:
                mean = ssum * (1.0 / w)
            else:
                mean = ssum / float(w)
            ypin[r0:r0 + ROW_CHUNK, gi * LANES:(gi + 1) * LANES] = (mean - cur).astype(BF16)
    for gi in range(len(POOL_WINDOWS)):
        cols = slice(gi * LANES, (gi + 1) * LANES)
        yo = jnp.dot(ypin[:, cols], wpool_ref[gi], preferred_element_type=F32) * spool_ref[:, cols]
        ycat[:, c_conv + gi * LANES:c_conv + (gi + 1) * LANES] = yo.astype(BF16)

    ubuf[:, 0:halo_u, :] = ubuf[:, ts:ts + halo_u, :]
    pbuf[:, 0:halo_p, :] = pbuf[:, ts:ts + halo_p, :]

    for c0 in range(0, d_model, MXU_N):
        cols = slice(c0, c0 + MXU_N)
        h_buf[:, cols] = x_ref[0, :, cols] + jnp.dot(ycat[...], wout_ref[:, cols],
                                                     preferred_element_type=F32)

    for r0 in range(0, ts, ROW_CHUNK):
        rows = slice(r0, r0 + ROW_CHUNK)
        xn_buf[rows, :] = _rmsnorm_rows(h_buf[rows, :], gffn_ref[...]).astype(BF16)

    for c0 in range(0, d_ff, MXU_N):
        cols = slice(c0, c0 + MXU_N)
        g = jnp.dot(xn_buf[...], wg_ref[:, cols], preferred_element_type=F32)
        u = jnp.dot(xn_buf[...], wu_ref[:, cols], preferred_element_type=F32)
        hid_buf[:, cols] = (g * jax.nn.sigmoid(g) * u).astype(BF16)

    for c0 in range(0, d_model, MXU_N):
        cols = slice(c0, c0 + MXU_N)
        h_buf[:, cols] = h_buf[:, cols] + jnp.dot(hid_buf[...], wd_ref[:, cols],
                                                  preferred_element_type=F32)

    for r0 in range(0, ts, ROW_CHUNK):
        rows = slice(r0, r0 + ROW_CHUNK)
        hr = h_buf[rows, :]
        o_ref[0, rows, :] = _rmsnorm_rows(hr, gfin_ref[...]) if apply_final else hr


def _resident(shape):
    zeros = (0,) * len(shape)
    return pl.BlockSpec(shape, lambda b, s: zeros, pipeline_mode=pl.Buffered(1))


def _layer(h, g_mix, w_in, b_in, w_dw, b_dw, ln_g, ln_b, w_pool, s_pool, w_out, g_ffn,
           w_gate, w_up, w_down, g_final, *, apply_final):
    batch, seq, d_model = h.shape
    conv_width, c_conv = w_dw.shape
    c_pool = s_pool.shape[0]
    d_ff = w_gate.shape[1]
    ts = SEQ_TILE
    halo_u = -(-(conv_width - 1) // ROW_CHUNK) * ROW_CHUNK
    halo_p = -(-(max(POOL_WINDOWS) - 1) // 16) * 16
    assert seq % ts == 0 and ts % ROW_CHUNK == 0 and ts % CONV_ROWS == 0 and ts >= halo_u
    assert ROW_CHUNK >= max(POOL_WINDOWS) - 1
    assert c_conv % MXU_N == 0 and c_pool % MXU_N == 0 and d_model % MXU_N == 0 and d_ff % MXU_N == 0
    assert c_pool == len(POOL_WINDOWS) * LANES

    row = lambda v: v.reshape(1, -1).astype(F32)
    operands = (
        h,
        row(g_mix), w_in.astype(BF16), row(b_in),
        w_dw.astype(F32), row(b_dw), row(ln_g), row(ln_b),
        w_pool.astype(BF16), row(s_pool),
        w_out.astype(BF16), row(g_ffn),
        w_gate.astype(BF16), w_up.astype(BF16), w_down.astype(BF16),
        row(g_final),
    )
    tile = pl.BlockSpec((1, ts, d_model), lambda b, s: (b, s, 0))
    in_specs = [tile] + [_resident(op.shape) for op in operands[1:]]

    body = functools.partial(_layer_kernel, apply_final=apply_final, conv_width=conv_width,
                             halo_u=halo_u, halo_p=halo_p)
    return pl.pallas_call(
        body,
        grid=(batch, seq // ts),
        in_specs=in_specs,
        out_specs=tile,
        out_shape=jax.ShapeDtypeStruct(h.shape, h.dtype),
        scratch_shapes=[
            pltpu.VMEM((ts, d_model), BF16),
            pltpu.VMEM((c_conv // LANES, halo_u + ts, LANES), F32),
            pltpu.VMEM((c_pool // LANES, halo_p + ts, LANES), F32),
            pltpu.VMEM((ts, c_pool), BF16),
            pltpu.VMEM((ts, c_conv + c_pool), BF16),
            pltpu.VMEM((ts, d_model), F32),
            pltpu.VMEM((ts, d_ff), BF16),
        ],
        compiler_params=pltpu.CompilerParams(
            dimension_semantics=("arbitrary", "arbitrary"),
            vmem_limit_bytes=VMEM_LIMIT_BYTES,
        ),
        name="hybrid_block_layer",
    )(*operands)


def kernel(x, g_mix, w_in, b_in, w_dw, b_dw, ln_g, ln_b, w_pool, s_pool, w_out, g_ffn,
           w_gate, w_up, w_down, g_final):
    depth = w_in.shape[0]
    assert depth >= 1
    h = x
    for l in range(depth):
        h = _layer(h, g_mix[l], w_in[l], b_in[l], w_dw[l], b_dw[l], ln_g[l], ln_b[l],
                   w_pool[l], s_pool[l], w_out[l], g_ffn[l], w_gate[l], w_up[l], w_down[l],
                   g_final, apply_final=(l == depth - 1))
    return h
```

```python
import functools

import jax
import jax.numpy as jnp
from jax import lax
from jax.experimental import pallas as pl
from jax.experimental.pallas import tpu as pltpu

RMS_EPS = 1e-6
LN_EPS = 1e-5
POOL_WINDOWS = (2, 4, 8, 16)

LANES = 128
MXU_N = 256
VMEM_LIMIT_BYTES = 56 * 1024 * 1024

SEQ_TILE = 512
ROW_CHUNK = 32
CONV_ROWS = 64

F32 = jnp.float32
BF16 = jnp.bfloat16


def _rmsnorm_rows(v, g):
    ms = jnp.mean(v * v, axis=-1, keepdims=True)
    return v * lax.rsqrt(ms + RMS_EPS) * g


def _emit_interleaved(first, stream_a, stream_b):
    for _, thunk in first:
        thunk()
    merged = []
    for tag, stream in ((0, stream_a), (1, stream_b)):
        total = float(sum(c for c, _ in stream))
        done = 0.0
        for i, (c, thunk) in enumerate(stream):
            merged.append(((done + 0.5 * c) / total, tag, i, thunk))
            done += c
    for _, _, _, thunk in sorted(merged, key=lambda m: m[:3]):
        thunk()


def _layer_kernel(xc_ref, xp_ref, gmix_ref, win_ref, bin_ref, wdw_ref, bdw_ref, lng_ref, lnb_ref,
                  wpool_ref, spool_ref, wout_ref, gffn_ref, wg_ref, wu_ref, wd_ref, gfin_ref,
                  o_ref,
                  xn_buf, ubuf, pbuf, ypin, ycat, h_buf, xn2_buf, hid_buf,
                  *, apply_final, conv_width, halo_u, halo_p, n_tiles, tiles_per_seq):
    ts = xc_ref.shape[1]
    d_model = xc_ref.shape[2]
    n_uslab, n_pslab = ubuf.shape[0], pbuf.shape[0]
    c_conv, c_pool = n_uslab * LANES, n_pslab * LANES
    d_ff = hid_buf.shape[1]
    step = pl.program_id(0)
    s_idx = lax.rem(jnp.minimum(step, n_tiles - 1), tiles_per_seq)

    @pl.when(s_idx == 0)
    def _():
        ubuf[:, 0:halo_u, :] = jnp.zeros((n_uslab, halo_u, LANES), F32)
        pbuf[:, 0:halo_p, :] = jnp.zeros((n_pslab, halo_p, LANES), F32)

    @pl.when(step == 0)
    def _():
        ycat[...] = jnp.zeros(ycat.shape, BF16)

    mix = []

    def prenorm(r0):
        rows = slice(r0, r0 + ROW_CHUNK)
        xn_buf[rows, :] = _rmsnorm_rows(xc_ref[0, rows, :], gmix_ref[...]).astype(BF16)

    def z_cols(c0):
        cols = slice(c0, c0 + MXU_N)
        return jnp.dot(xn_buf[...], win_ref[:, cols], preferred_element_type=F32) + bin_ref[:, cols]

    def store_slabs(buf, halo, c0, val):
        for l0 in range(0, MXU_N, LANES):
            buf[(c0 + l0) // LANES, halo:halo + ts, :] = val[:, l0:l0 + LANES]

    def glu(sl):
        la = slice(sl * LANES, (sl + 1) * LANES)
        lg = slice(c_conv + sl * LANES, c_conv + (sl + 1) * LANES)
        w = jnp.concatenate([win_ref[:, la], win_ref[:, lg]], axis=1)
        b = jnp.concatenate([bin_ref[:, la], bin_ref[:, lg]], axis=1)
        z = jnp.dot(xn_buf[...], w, preferred_element_type=F32) + b
        ubuf[sl, halo_u:halo_u + ts, :] = z[:, :LANES] * jax.nn.sigmoid(z[:, LANES:])

    def pool_in(c0):
        store_slabs(pbuf, halo_p, c0, z_cols(2 * c_conv + c0))

    base = halo_u - (conv_width - 1)

    def conv_ln_swish(r0):
        slabs = []
        for sl in range(n_uslab):
            lanes = slice(sl * LANES, (sl + 1) * LANES)
            acc = None
            for k in range(conv_width):
                term = wdw_ref[k:k + 1, lanes] * ubuf[sl, pl.ds(r0 + base + k, CONV_ROWS), :]
                acc = term if acc is None else acc + term
            slabs.append(acc)
        v = jnp.concatenate(slabs, axis=1) + bdw_ref[...]
        mu = jnp.mean(v, axis=-1, keepdims=True)
        dv = v - mu
        var = jnp.mean(dv * dv, axis=-1, keepdims=True)
        yn = dv * lax.rsqrt(var + LN_EPS) * lng_ref[...] + lnb_ref[...]
        ycat[r0:r0 + CONV_ROWS, 0:c_conv] = (yn * jax.nn.sigmoid(yn)).astype(BF16)

    def pool_rows(r0):
        for gi, w in enumerate(POOL_WINDOWS):
            cur = pbuf[gi, pl.ds(halo_p + r0, ROW_CHUNK), :]
            ssum = cur
            for j in range(1, w):
                ssum = ssum + pbuf[gi, pl.ds(halo_p + r0 - j, ROW_CHUNK), :]
            if r0 == 0:
                pos = s_idx * ts + lax.broadcasted_iota(jnp.int32, (ROW_CHUNK, LANES), 0)
                mean = ssum / jnp.minimum(pos + 1, w).astype(F32)
            elif w & (w - 1) == 0:
                mean = ssum * (1.0 / w)
            else:
                mean = ssum / float(w)
            ypin[r0:r0 + ROW_CHUNK, gi * LANES:(gi + 1) * LANES] = (mean - cur).astype(BF16)

    def pool_map(gi):
        cols = slice(gi * LANES, (gi + 1) * LANES)
        yo = jnp.dot(ypin[:, cols], wpool_ref[gi], preferred_element_type=F32) * spool_ref[:, cols]
        ycat[:, c_conv + gi * LANES:c_conv + (gi + 1) * LANES] = yo.astype(BF16)

    def carry_history():
        ubuf[:, 0:halo_u, :] = ubuf[:, ts:ts + halo_u, :]
        pbuf[:, 0:halo_p, :] = pbuf[:, ts:ts + halo_p, :]

    for r0 in range(0, ts, ROW_CHUNK):
        mix.append((40, functools.partial(prenorm, r0)))
    for sl in range(n_uslab):
        mix.append((600, functools.partial(glu, sl)))
    for c0 in range(0, c_pool, MXU_N):
        mix.append((500, functools.partial(pool_in, c0)))
    for i, r0 in enumerate(range(0, ts, CONV_ROWS)):
        mix.append((600, functools.partial(conv_ln_swish, r0)))
        for rp in range(r0, r0 + CONV_ROWS, ROW_CHUNK):
            mix.append((40, functools.partial(pool_rows, rp)))
    for gi in range(len(POOL_WINDOWS)):
        mix.append((130, functools.partial(pool_map, gi)))
    mix.append((10, carry_history))

    def out_proj(c0):
        cols = slice(c0, c0 + MXU_N)
        h_buf[:, cols] = xp_ref[0, :, cols] + jnp.dot(ycat[...], wout_ref[:, cols],
                                                      preferred_element_type=F32)

    def ffn_prenorm(r0):
        rows = slice(r0, r0 + ROW_CHUNK)
        xn2_buf[rows, :] = _rmsnorm_rows(h_buf[rows, :], gffn_ref[...]).astype(BF16)

    def swiglu(c0):
        cols = slice(c0, c0 + LANES)
        w = jnp.concatenate([wg_ref[:, cols], wu_ref[:, cols]], axis=1)
        gu = jnp.dot(xn2_buf[...], w, preferred_element_type=F32)
        g, u = gu[:, :LANES], gu[:, LANES:]
        hid_buf[:, cols] = (g * jax.nn.sigmoid(g) * u).astype(BF16)

    def down_proj(c0):
        cols = slice(c0, c0 + MXU_N)
        h_buf[:, cols] = h_buf[:, cols] + jnp.dot(hid_buf[...], wd_ref[:, cols],
                                                  preferred_element_type=F32)

    def close(r0):
        rows = slice(r0, r0 + ROW_CHUNK)
        hr = h_buf[rows, :]
        o_ref[0, rows, :] = _rmsnorm_rows(hr, gfin_ref[...]) if apply_final else hr

    head = [(500, functools.partial(out_proj, c0)) for c0 in range(0, d_model, MXU_N)]
    ffn = []
    for r0 in range(0, ts, ROW_CHUNK):
        ffn.append((40, functools.partial(ffn_prenorm, r0)))
    for c0 in range(0, d_ff, LANES):
        ffn.append((550, functools.partial(swiglu, c0)))
    for c0 in range(0, d_model, MXU_N):
        ffn.append((1400, functools.partial(down_proj, c0)))
    for r0 in range(0, ts, ROW_CHUNK):
        ffn.append((40, functools.partial(close, r0)))

    _emit_interleaved(head, mix, ffn)


def _resident(shape):
    zeros = (0,) * len(shape)
    return pl.BlockSpec(shape, lambda n: zeros, pipeline_mode=pl.Buffered(1))


def _layer(h, g_mix, w_in, b_in, w_dw, b_dw, ln_g, ln_b, w_pool, s_pool, w_out, g_ffn,
           w_gate, w_up, w_down, g_final, *, apply_final):
    batch, seq, d_model = h.shape
    conv_width, c_conv = w_dw.shape
    c_pool = s_pool.shape[0]
    d_ff = w_gate.shape[1]
    ts = SEQ_TILE
    halo_u = -(-(conv_width - 1) // ROW_CHUNK) * ROW_CHUNK
    halo_p = -(-(max(POOL_WINDOWS) - 1) // 16) * 16
    assert seq % ts == 0 and ts % ROW_CHUNK == 0 and ts % CONV_ROWS == 0 and ts >= halo_u
    assert CONV_ROWS % ROW_CHUNK == 0
    assert ROW_CHUNK >= max(POOL_WINDOWS) - 1
    assert c_conv % MXU_N == 0 and c_pool % MXU_N == 0 and d_model % MXU_N == 0 and d_ff % MXU_N == 0
    assert c_pool == len(POOL_WINDOWS) * LANES
    tiles_per_seq = seq // ts
    n_tiles = batch * tiles_per_seq

    row = lambda v: v.reshape(1, -1).astype(F32)
    weights = (
        row(g_mix), w_in.astype(BF16), row(b_in),
        w_dw.astype(F32), row(b_dw), row(ln_g), row(ln_b),
        w_pool.astype(BF16), row(s_pool),
        w_out.astype(BF16), row(g_ffn),
        w_gate.astype(BF16), w_up.astype(BF16), w_down.astype(BF16),
        row(g_final),
    )

    def tile_spec(tile_of_step):
        def index_map(n):
            t = tile_of_step(n)
            return (t // tiles_per_seq, t % tiles_per_seq, 0)
        return pl.BlockSpec((1, ts, d_model), index_map)

    cur_tile = tile_spec(lambda n: jnp.minimum(n, n_tiles - 1))
    prev_tile = tile_spec(lambda n: jnp.maximum(n - 1, 0))

    body = functools.partial(_layer_kernel, apply_final=apply_final, conv_width=conv_width,
                             halo_u=halo_u, halo_p=halo_p, n_tiles=n_tiles,
                             tiles_per_seq=tiles_per_seq)
    return pl.pallas_call(
        body,
        grid=(n_tiles + 1,),
        in_specs=[cur_tile, prev_tile] + [_resident(w.shape) for w in weights],
        out_specs=prev_tile,
        out_shape=jax.ShapeDtypeStruct(h.shape, h.dtype),
        scratch_shapes=[
            pltpu.VMEM((ts, d_model), BF16),
            pltpu.VMEM((c_conv // LANES, halo_u + ts, LANES), F32),
            pltpu.VMEM((c_pool // LANES, halo_p + ts, LANES), F32),
            pltpu.VMEM((ts, c_pool), BF16),
            pltpu.VMEM((ts, c_conv + c_pool), BF16),
            pltpu.VMEM((ts, d_model), F32),
            pltpu.VMEM((ts, d_model), BF16),
            pltpu.VMEM((ts, d_ff), BF16),
        ],
        compiler_params=pltpu.CompilerParams(
            dimension_semantics=("arbitrary",),
            vmem_limit_bytes=VMEM_LIMIT_BYTES,
        ),
        name="hybrid_block_layer",
    )(h, h, *weights)


def kernel(x, g_mix, w_in, b_in, w_dw, b_dw, ln_g, ln_b, w_pool, s_pool, w_out, g_ffn,
           w_gate, w_up, w_down, g_final):
    depth = w_in.shape[0]
    assert depth >= 1
    h = x
    for l in range(depth):
        h = _layer(h, g_mix[l], w_in[l], b_in[l], w_dw[l], b_dw[l], ln_g[l], ln_b[l],
                   w_pool[l], s_pool[l], w_out[l], g_ffn[l], w_gate[l], w_up[l], w_down[l],
                   g_final, apply_final=(l == depth - 1))
    return h
```

```python
import functools

import jax
import jax.numpy as jnp
from jax import lax
from jax.experimental import pallas as pl
from jax.experimental.pallas import tpu as pltpu

RMS_EPS = 1e-6
LN_EPS = 1e-5
POOL_WINDOWS = (2, 4, 8, 16)

LANES = 128
MXU_N = 256
VMEM_LIMIT_BYTES = 56 * 1024 * 1024

SEQ_TILE = 512
ROW_CHUNK = 32
CONV_ROWS = 64

F32 = jnp.float32
BF16 = jnp.bfloat16


def _rmsnorm_rows(v, g):
    ms = jnp.mean(v * v, axis=-1, keepdims=True)
    return v * lax.rsqrt(ms + RMS_EPS) * g


def _emit_interleaved(stream_a, stream_b):
    merged = []
    for tag, stream in ((0, stream_a), (1, stream_b)):
        total = float(sum(c for c, _ in stream))
        done = 0.0
        for i, (c, thunk) in enumerate(stream):
            merged.append(((done + 0.5 * c) / total, tag, i, thunk))
            done += c
    for _, _, _, thunk in sorted(merged, key=lambda m: m[:3]):
        thunk()


def _layer_kernel(xc_ref, gmix_ref, win_ref, bin_ref, wdw_ref, bdw_ref, lng_ref, lnb_ref,
                  wpool_ref, spool_ref, wout_ref, gffn_ref, wg_ref, wu_ref, wd_ref, gfin_ref,
                  o_ref,
                  xn_buf, ubuf, pbuf, ypin, ycat, hn_buf, xn2_buf, hid_buf, h_buf,
                  *, apply_final, conv_width, halo_u, halo_p, n_tiles, tiles_per_seq):
    ts = xc_ref.shape[1]
    d_model = xc_ref.shape[2]
    n_uslab, n_pslab = ubuf.shape[0], pbuf.shape[0]
    c_conv, c_pool = n_uslab * LANES, n_pslab * LANES
    d_ff = hid_buf.shape[1]
    step = pl.program_id(0)
    s_idx = lax.rem(jnp.minimum(step, n_tiles - 1), tiles_per_seq)

    @pl.when(s_idx == 0)
    def _():
        ubuf[:, 0:halo_u, :] = jnp.zeros((n_uslab, halo_u, LANES), F32)
        pbuf[:, 0:halo_p, :] = jnp.zeros((n_pslab, halo_p, LANES), F32)

    @pl.when(step == 0)
    def _():
        hn_buf[...] = jnp.zeros(hn_buf.shape, F32)
        xn2_buf[...] = jnp.zeros(xn2_buf.shape, BF16)

    mix = []

    def prenorm(r0):
        rows = slice(r0, r0 + ROW_CHUNK)
        xn_buf[rows, :] = _rmsnorm_rows(xc_ref[0, rows, :], gmix_ref[...]).astype(BF16)

    def z_cols(c0):
        cols = slice(c0, c0 + MXU_N)
        return jnp.dot(xn_buf[...], win_ref[:, cols], preferred_element_type=F32) + bin_ref[:, cols]

    def store_slabs(buf, halo, c0, val):
        for l0 in range(0, MXU_N, LANES):
            buf[(c0 + l0) // LANES, halo:halo + ts, :] = val[:, l0:l0 + LANES]

    def glu(sl):
        la = slice(sl * LANES, (sl + 1) * LANES)
        lg = slice(c_conv + sl * LANES, c_conv + (sl + 1) * LANES)
        w = jnp.concatenate([win_ref[:, la], win_ref[:, lg]], axis=1)
        b = jnp.concatenate([bin_ref[:, la], bin_ref[:, lg]], axis=1)
        z = jnp.dot(xn_buf[...], w, preferred_element_type=F32) + b
        ubuf[sl, halo_u:halo_u + ts, :] = z[:, :LANES] * jax.nn.sigmoid(z[:, LANES:])

    def pool_in(c0):
        store_slabs(pbuf, halo_p, c0, z_cols(2 * c_conv + c0))

    base = halo_u - (conv_width - 1)

    def conv_ln_swish(r0):
        slabs = []
        for sl in range(n_uslab):
            lanes = slice(sl * LANES, (sl + 1) * LANES)
            acc = None
            for k in range(conv_width):
                term = wdw_ref[k:k + 1, lanes] * ubuf[sl, pl.ds(r0 + base + k, CONV_ROWS), :]
                acc = term if acc is None else acc + term
            slabs.append(acc)
        v = jnp.concatenate(slabs, axis=1) + bdw_ref[...]
        mu = jnp.mean(v, axis=-1, keepdims=True)
        dv = v - mu
        var = jnp.mean(dv * dv, axis=-1, keepdims=True)
        yn = dv * lax.rsqrt(var + LN_EPS) * lng_ref[...] + lnb_ref[...]
        ycat[r0:r0 + CONV_ROWS, 0:c_conv] = (yn * jax.nn.sigmoid(yn)).astype(BF16)

    def pool_rows(r0):
        for gi, w in enumerate(POOL_WINDOWS):
            cur = pbuf[gi, pl.ds(halo_p + r0, ROW_CHUNK), :]
            ssum = cur
            for j in range(1, w):
                ssum = ssum + pbuf[gi, pl.ds(halo_p + r0 - j, ROW_CHUNK), :]
            if r0 == 0:
                pos = s_idx * ts + lax.broadcasted_iota(jnp.int32, (ROW_CHUNK, LANES), 0)
                mean = ssum / jnp.minimum(pos + 1, w).astype(F32)
            elif w & (w - 1) == 0:
                mean = ssum * (1.0 / w)
            else:
                mean = ssum / float(w)
            ypin[r0:r0 + ROW_CHUNK, gi * LANES:(gi + 1) * LANES] = (mean - cur).astype(BF16)

    def pool_map(gi):
        cols = slice(gi * LANES, (gi + 1) * LANES)
        yo = jnp.dot(ypin[:, cols], wpool_ref[gi], preferred_element_type=F32) * spool_ref[:, cols]
        ycat[:, c_conv + gi * LANES:c_conv + (gi + 1) * LANES] = yo.astype(BF16)

    def carry_history():
        ubuf[:, 0:halo_u, :] = ubuf[:, ts:ts + halo_u, :]
        pbuf[:, 0:halo_p, :] = pbuf[:, ts:ts + halo_p, :]

    for r0 in range(0, ts, ROW_CHUNK):
        mix.append((40, functools.partial(prenorm, r0)))
    for sl in range(n_uslab):
        mix.append((600, functools.partial(glu, sl)))
    for c0 in range(0, c_pool, MXU_N):
        mix.append((500, functools.partial(pool_in, c0)))
    for r0 in range(0, ts, CONV_ROWS):
        mix.append((600, functools.partial(conv_ln_swish, r0)))
        for rp in range(r0, r0 + CONV_ROWS, ROW_CHUNK):
            mix.append((40, functools.partial(pool_rows, rp)))
    for gi in range(len(POOL_WINDOWS)):
        mix.append((130, functools.partial(pool_map, gi)))
    mix.append((10, carry_history))

    def out_proj(c0):
        cols = slice(c0, c0 + MXU_N)
        hn_buf[:, cols] = xc_ref[0, :, cols] + jnp.dot(ycat[...], wout_ref[:, cols],
                                                       preferred_element_type=F32)

    def ffn_prenorm(r0):
        rows = slice(r0, r0 + ROW_CHUNK)
        xn2_buf[rows, :] = _rmsnorm_rows(hn_buf[rows, :], gffn_ref[...]).astype(BF16)

    def swiglu(c0):
        cols = slice(c0, c0 + LANES)
        w = jnp.concatenate([wg_ref[:, cols], wu_ref[:, cols]], axis=1)
        gu = jnp.dot(xn2_buf[...], w, preferred_element_type=F32)
        g, u = gu[:, :LANES], gu[:, LANES:]
        hid_buf[:, cols] = (g * jax.nn.sigmoid(g) * u).astype(BF16)

    def down_proj(c0):
        cols = slice(c0, c0 + MXU_N)
        h_buf[:, cols] = hn_buf[:, cols] + jnp.dot(hid_buf[...], wd_ref[:, cols],
                                                   preferred_element_type=F32)

    def close(r0):
        rows = slice(r0, r0 + ROW_CHUNK)
        hr = h_buf[rows, :]
        o_ref[0, rows, :] = _rmsnorm_rows(hr, gfin_ref[...]) if apply_final else hr

    ffn = []
    for c0 in range(0, d_ff, LANES):
        ffn.append((550, functools.partial(swiglu, c0)))
    for c0 in range(0, d_model, MXU_N):
        ffn.append((1400, functools.partial(down_proj, c0)))
    _emit_interleaved(mix, ffn)

    handover = [(500, functools.partial(out_proj, c0)) for c0 in range(0, d_model, MXU_N)]
    handover += [(40, functools.partial(ffn_prenorm, r0)) for r0 in range(0, ts, ROW_CHUNK)]
    closing = [(40, functools.partial(close, r0)) for r0 in range(0, ts, ROW_CHUNK)]
    _emit_interleaved(closing, handover)


def _resident(shape):
    zeros = (0,) * len(shape)
    return pl.BlockSpec(shape, lambda n: zeros, pipeline_mode=pl.Buffered(1))


def _layer(h, g_mix, w_in, b_in, w_dw, b_dw, ln_g, ln_b, w_pool, s_pool, w_out, g_ffn,
           w_gate, w_up, w_down, g_final, *, apply_final):
    batch, seq, d_model = h.shape
    conv_width, c_conv = w_dw.shape
    c_pool = s_pool.shape[0]
    d_ff = w_gate.shape[1]
    ts = SEQ_TILE
    halo_u = -(-(conv_width - 1) // ROW_CHUNK) * ROW_CHUNK
    halo_p = -(-(max(POOL_WINDOWS) - 1) // 16) * 16
    assert seq % ts == 0 and ts % ROW_CHUNK == 0 and ts % CONV_ROWS == 0 and ts >= halo_u
    assert CONV_ROWS % ROW_CHUNK == 0
    assert ROW_CHUNK >= max(POOL_WINDOWS) - 1
    assert c_conv % MXU_N == 0 and c_pool % MXU_N == 0 and d_model % MXU_N == 0 and d_ff % MXU_N == 0
    assert c_pool == len(POOL_WINDOWS) * LANES
    tiles_per_seq = seq // ts
    n_tiles = batch * tiles_per_seq

    row = lambda v: v.reshape(1, -1).astype(F32)
    weights = (
        row(g_mix), w_in.astype(BF16), row(b_in),
        w_dw.astype(F32), row(b_dw), row(ln_g), row(ln_b),
        w_pool.astype(BF16), row(s_pool),
        w_out.astype(BF16), row(g_ffn),
        w_gate.astype(BF16), w_up.astype(BF16), w_down.astype(BF16),
        row(g_final),
    )

    def tile_spec(tile_of_step):
        def index_map(n):
            t = tile_of_step(n)
            return (t // tiles_per_seq, t % tiles_per_seq, 0)
        return pl.BlockSpec((1, ts, d_model), index_map)

    cur_tile = tile_spec(lambda n: jnp.minimum(n, n_tiles - 1))
    prev_tile = tile_spec(lambda n: jnp.maximum(n - 1, 0))

    body = functools.partial(_layer_kernel, apply_final=apply_final, conv_width=conv_width,
                             halo_u=halo_u, halo_p=halo_p, n_tiles=n_tiles,
                             tiles_per_seq=tiles_per_seq)
    return pl.pallas_call(
        body,
        grid=(n_tiles + 1,),
        in_specs=[cur_tile] + [_resident(w.shape) for w in weights],
        out_specs=prev_tile,
        out_shape=jax.ShapeDtypeStruct(h.shape, h.dtype),
        scratch_shapes=[
            pltpu.VMEM((ts, d_model), BF16),
            pltpu.VMEM((c_conv // LANES, halo_u + ts, LANES), F32),
            pltpu.VMEM((c_pool // LANES, halo_p + ts, LANES), F32),
            pltpu.VMEM((ts, c_pool), BF16),
            pltpu.VMEM((ts, c_conv + c_pool), BF16),
            pltpu.VMEM((ts, d_model), F32),
            pltpu.VMEM((ts, d_model), BF16),
            pltpu.VMEM((ts, d_ff), BF16),
            pltpu.VMEM((ts, d_model), F32),
        ],
        compiler_params=pltpu.CompilerParams(
            dimension_semantics=("arbitrary",),
            vmem_limit_bytes=VMEM_LIMIT_BYTES,
        ),
        name="hybrid_block_layer",
    )(h, *weights)


def kernel(x, g_mix, w_in, b_in, w_dw, b_dw, ln_g, ln_b, w_pool, s_pool, w_out, g_ffn,
           w_gate, w_up, w_down, g_final):
    depth = w_in.shape[0]
    assert depth >= 1
    h = x
    for l in range(depth):
        h = _layer(h, g_mix[l], w_in[l], b_in[l], w_dw[l], b_dw[l], ln_g[l], ln_b[l],
                   w_pool[l], s_pool[l], w_out[l], g_ffn[l], w_gate[l], w_up[l], w_down[l],
                   g_final, apply_final=(l == depth - 1))
    return h
```
